```python
import jax, jax.numpy as jnp
from jax import lax
import numpy as np

D_MODEL = 1024
BATCH = 4
SEQ = 8192
DEPTH = 1

GRID_W = 64
CTX_LEN = 256
NORM_EPS = 1e-6
SSD_EXPAND = 2
D_INNER = SSD_EXPAND * D_MODEL
SSD_HEAD_DIM = 64
SSD_HEADS = D_INNER // SSD_HEAD_DIM
SSD_GROUPS = 4
SSD_STATE = 128
SSD_CONV_W = 5
SSD_CHUNK = 128
XBC_DIM = D_INNER + 2 * SSD_GROUPS * SSD_STATE
ATTN_HEAD_DIM = 64
ATTN_HEADS = 16
ATTN_KV_HEADS = 4
ATTN_DIM = ATTN_HEADS * ATTN_HEAD_DIM
KV_DIM = ATTN_KV_HEADS * ATTN_HEAD_DIM
Q_BLOCK = 128
ROPE_THETA = 10000.0
N_BRANCHES = 2
IN_SIZES = (D_INNER, XBC_DIM, 2 * SSD_HEADS, ATTN_DIM, KV_DIM, KV_DIM, ATTN_DIM, D_MODEL, D_MODEL)
IN_DIM = sum(IN_SIZES)

kernel_name = "hybrid_ssd_gqa_prefix_ctx_dit_block"


def rmsnorm(x, g):
    xf = x.astype(jnp.float32)
    y = xf * lax.rsqrt(jnp.mean(xf * xf, axis=-1, keepdims=True) + NORM_EPS)
    return (y * g.astype(jnp.float32)).astype(x.dtype)


def split_proj(p):
    idx = [int(v) for v in np.cumsum(IN_SIZES)[:-1]]
    return jnp.split(p, idx, axis=-1)


def centred_dwconv(u, w, b):
    pad = w.shape[0] // 2
    out = lax.conv_general_dilated(u, w[:, None, :].astype(u.dtype), window_strides=(1,),
                                   padding=((pad, pad),), dimension_numbers=("NWC", "WIO", "NWC"),
                                   feature_group_count=u.shape[-1])
    return out + b


def ssd_scan(x, dt, a, B, C, s0):
    b, l, h, p = x.shape
    g, n = B.shape[2], B.shape[3]
    r = h // g
    q = SSD_CHUNK
    nc = l // q
    xc = x.astype(jnp.float32).reshape(b, nc, q, g, r, p)
    dtc = dt.astype(jnp.float32).reshape(b, nc, q, g, r)
    Bc = B.astype(jnp.float32).reshape(b, nc, q, g, n)
    Cc = C.astype(jnp.float32).reshape(b, nc, q, g, n)
    a_cum = jnp.cumsum(dtc * a.reshape(g, r), axis=2)
    a_last = a_cum[:, :, -1]
    seg = a_cum[:, :, :, None] - a_cum[:, :, None, :]
    lower = jnp.tril(jnp.ones((q, q), dtype=bool))[:, :, None, None]
    decay = jnp.exp(jnp.where(lower, seg, -jnp.inf))
    cb = jnp.einsum("bcign,bcjgn->bcijg", Cc, Bc)
    y_diag = jnp.einsum("bcijg,bcijgr,bcjgr,bcjgrp->bcigrp", cb, decay, dtc, xc)
    w_state = jnp.exp(a_last[:, :, None] - a_cum) * dtc
    states = jnp.einsum("bcjgn,bcjgr,bcjgrp->bcgrpn", Bc, w_state, xc)
    chunk_decay = jnp.exp(a_last)

    def step(s, inp):
        dec, st = inp
        return dec[..., None, None] * s + st, s

    s_final, s_prev = lax.scan(step, s0.astype(jnp.float32).reshape(b, g, r, p, n),
                               (jnp.moveaxis(chunk_decay, 1, 0), jnp.moveaxis(states, 1, 0)))
    s_prev = jnp.moveaxis(s_prev, 0, 1)
    y_off = jnp.einsum("bcign,bcgrpn,bcigr->bcigrp", Cc, s_prev, jnp.exp(a_cum))
    y = (y_diag + y_off).reshape(b, l, h, p)
    return y, s_final.reshape(b, h, p, n)


def bidir_ssd(xc, dtc, bc, cc, xl, dtl, bl, cl, dt_bias, a_log):
    b, _, h, p = xl.shape
    n = bl.shape[-1]
    s0 = jnp.zeros((b, h, p, n), jnp.float32)
    y_ctx = 0.0
    y_lat = 0.0
    for d in range(2):
        f = (lambda t: jnp.flip(t, axis=1)) if d == 1 else (lambda t: t)
        a = -jnp.exp(a_log[d].astype(jnp.float32))
        dt_c = jax.nn.softplus((dtc[:, :, d] + dt_bias[d]).astype(jnp.float32))
        dt_l = jax.nn.softplus((dtl[:, :, d] + dt_bias[d]).astype(jnp.float32))
        yc, s_ctx = ssd_scan(f(xc), f(dt_c), a, f(bc), f(cc), s0)
        yl, _ = ssd_scan(f(xl), f(dt_l), a, f(bl), f(cl), s_ctx)
        y_ctx = y_ctx + f(yc)
        y_lat = y_lat + f(yl)
    return y_ctx.astype(xc.dtype), y_lat.astype(xl.dtype)


def axial_rope(t, rows):
    d = t.shape[-1]
    half = d // 2
    nf = half // 2
    row = jnp.repeat(jnp.arange(rows, dtype=jnp.float32), GRID_W)
    col = jnp.tile(jnp.arange(GRID_W, dtype=jnp.float32), rows)
    inv_freq = ROPE_THETA ** (-(jnp.arange(nf, dtype=jnp.float32) / nf))

    def rot(u, pos):
        ang = pos[:, None] * inv_freq
        cos = jnp.cos(ang)[None, :, None, :].astype(u.dtype)
        sin = jnp.sin(ang)[None, :, None, :].astype(u.dtype)
        u1, u2 = u[..., :nf], u[..., nf:]
        return jnp.concatenate([u1 * cos - u2 * sin, u2 * cos + u1 * sin], axis=-1)

    return jnp.concatenate([rot(t[..., :half], row), rot(t[..., half:], col)], axis=-1)


def block_attention(q, k, v):
    b, l, kvh, r, d = q.shape
    nb = l // Q_BLOCK
    qb = jnp.moveaxis(q.reshape(b, nb, Q_BLOCK, kvh, r, d), 1, 0)
    scale = d ** -0.5

    def one(qi):
        s = jnp.einsum("bqgrd,bkgd->bgrqk", qi, k).astype(jnp.float32) * scale
        pr = jax.nn.softmax(s, axis=-1).astype(v.dtype)
        return jnp.einsum("bgrqk,bkgd->bqgrd", pr, v)

    o = lax.map(one, qb)
    return jnp.moveaxis(o, 0, 1).reshape(b, l, kvh * r * d)


def merge_out(y_ssd, xs, z, o, ga, gs, gatt, d_skip, ssd_norm_g, w_ssd_out, w_attn_out, w_out):
    b, l = xs.shape[0], xs.shape[1]
    y = (y_ssd + d_skip[:, None] * xs).reshape(b, l, D_INNER)
    u = (y * jax.nn.silu(z)).reshape(b, l, SSD_GROUPS, D_INNER // SSD_GROUPS)
    u = rmsnorm(u, ssd_norm_g.reshape(SSD_GROUPS, -1)).reshape(b, l, D_INNER)
    br_ssd = u @ w_ssd_out
    br_att = (o * jax.nn.silu(ga)) @ w_attn_out
    m = jax.nn.sigmoid(gs) * br_ssd + jax.nn.sigmoid(gatt) * br_att
    return m @ w_out


def hybrid_layer(x, ctx, c, c_ctx, w_ada, b_ada, norm_g, w_in, conv_w, conv_b, dt_bias, a_log,
                 d_skip, ssd_norm_g, w_ssd_out, q_norm_g, k_norm_g, w_attn_out, w_out, update_ctx):
    b, l, _ = x.shape
    lc = ctx.shape[1]
    rows = l // GRID_W
    mod = jax.nn.silu(c) @ w_ada + b_ada
    shift, scale, gate = jnp.split(mod[:, None, :], 3, axis=-1)
    mod_c = jax.nn.silu(c_ctx) @ w_ada + b_ada
    shift_c, scale_c, gate_c = jnp.split(mod_c, 3, axis=-1)
    hx = rmsnorm(x, norm_g) * (1.0 + scale) + shift
    hc = rmsnorm(ctx, norm_g) * (1.0 + scale_c) + shift_c
    z_x, xbc_x, dt_x, q_x, k_x, v_x, ga_x, gs_x, gatt_x = split_proj(hx @ w_in)
    z_c, xbc_c, dt_c, q_c, k_c, v_c, ga_c, gs_c, gatt_c = split_proj(hc @ w_in)

    def ssd_inputs(xbc, dt, ln):
        xbc = jax.nn.silu(centred_dwconv(xbc, conv_w, conv_b))
        xs, Bs, Cs = jnp.split(xbc, [D_INNER, D_INNER + SSD_GROUPS * SSD_STATE], axis=-1)
        return (xs.reshape(b, ln, SSD_HEADS, SSD_HEAD_DIM), dt.reshape(b, ln, 2, SSD_HEADS),
                Bs.reshape(b, ln, SSD_GROUPS, SSD_STATE), Cs.reshape(b, ln, SSD_GROUPS, SSD_STATE))

    xs_c, dts_c, B_c, C_c = ssd_inputs(xbc_c, dt_c, lc)
    xs_x, dts_x, B_x, C_x = ssd_inputs(xbc_x, dt_x, l)
    y_ctx, y_lat = bidir_ssd(xs_c, dts_c, B_c, C_c, xs_x, dts_x, B_x, C_x, dt_bias, a_log)

    rep = ATTN_HEADS // ATTN_KV_HEADS
    q_l = axial_rope(rmsnorm(q_x.reshape(b, l, ATTN_HEADS, ATTN_HEAD_DIM), q_norm_g), rows)
    k_l = axial_rope(rmsnorm(k_x.reshape(b, l, ATTN_KV_HEADS, ATTN_HEAD_DIM), k_norm_g), rows)
    k_cx = rmsnorm(k_c.reshape(b, lc, ATTN_KV_HEADS, ATTN_HEAD_DIM), k_norm_g)
    v_l = v_x.reshape(b, l, ATTN_KV_HEADS, ATTN_HEAD_DIM)
    v_cx = v_c.reshape(b, lc, ATTN_KV_HEADS, ATTN_HEAD_DIM)
    k_all = jnp.concatenate([k_cx, k_l], axis=1)
    v_all = jnp.concatenate([v_cx, v_l], axis=1)
    o_lat = block_attention(q_l.reshape(b, l, ATTN_KV_HEADS, rep, ATTN_HEAD_DIM), k_all, v_all)

    out = merge_out(y_lat, xs_x, z_x, o_lat, ga_x, gs_x, gatt_x, d_skip, ssd_norm_g,
                    w_ssd_out, w_attn_out, w_out)
    x_new = x + gate * out

    ctx_new = ctx
    if update_ctx:
        q_cx = rmsnorm(q_c.reshape(b, lc, ATTN_HEADS, ATTN_HEAD_DIM), q_norm_g)
        o_ctx = block_attention(q_cx.reshape(b, lc, ATTN_KV_HEADS, rep, ATTN_HEAD_DIM), k_cx, v_cx)
        out_c = merge_out(y_ctx, xs_c, z_c, o_ctx, ga_c, gs_c, gatt_c, d_skip, ssd_norm_g,
                          w_ssd_out, w_attn_out, w_out)
        ctx_new = ctx + gate_c * out_c
    return x_new, ctx_new


def setup_inputs(seed: int = 0) -> dict:
    key = jax.random.key(seed)
    ks = jax.random.split(key, 24)
    f32 = jnp.float32
    nrm = lambda k, shape, s: jax.random.normal(k, shape, f32) * s
    dt0 = jnp.exp(jax.random.uniform(ks[10], (DEPTH, 2, SSD_HEADS), f32)
                  * (np.log(0.1) - np.log(0.001)) + np.log(0.001))
    dt_bias = dt0 + jnp.log(-jnp.expm1(-dt0))
    a_log = jnp.log(jax.random.uniform(ks[11], (DEPTH, 2, SSD_HEADS), f32, 1.0, 16.0))
    return {
        "x": nrm(ks[0], (BATCH, SEQ, D_MODEL), 1.0),
        "c": nrm(ks[1], (BATCH, D_MODEL), 1.0),
        "ctx": nrm(ks[2], (BATCH, CTX_LEN, D_MODEL), 1.0),
        "c_ctx": nrm(ks[3], (D_MODEL,), 1.0),
        "w_ada": nrm(ks[4], (DEPTH, D_MODEL, 3 * D_MODEL), 0.5 * D_MODEL ** -0.5),
        "b_ada": nrm(ks[5], (DEPTH, 3 * D_MODEL), 0.02),
        "norm_g": 1.0 + nrm(ks[6], (DEPTH, D_MODEL), 0.05),
        "w_in": nrm(ks[7], (DEPTH, D_MODEL, IN_DIM), D_MODEL ** -0.5),
        "conv_w": nrm(ks[8], (DEPTH, SSD_CONV_W, XBC_DIM), SSD_CONV_W ** -0.5),
        "conv_b": nrm(ks[9], (DEPTH, XBC_DIM), 0.02),
        "dt_bias": dt_bias,
        "a_log": a_log,
        "d_skip": 1.0 + nrm(ks[12], (DEPTH, SSD_HEADS), 0.05),
        "ssd_norm_g": 1.0 + nrm(ks[13], (DEPTH, D_INNER), 0.05),
        "w_ssd_out": nrm(ks[14], (DEPTH, D_INNER, D_MODEL), D_INNER ** -0.5),
        "q_norm_g": 1.0 + nrm(ks[15], (DEPTH, ATTN_HEAD_DIM), 0.05),
        "k_norm_g": 1.0 + nrm(ks[16], (DEPTH, ATTN_HEAD_DIM), 0.05),
        "w_attn_out": nrm(ks[17], (DEPTH, ATTN_DIM, D_MODEL), ATTN_DIM ** -0.5),
        "w_out": nrm(ks[18], (DEPTH, D_MODEL, D_MODEL), D_MODEL ** -0.5),
    }


def reference(x, c, ctx, c_ctx, w_ada, b_ada, norm_g, w_in, conv_w, conv_b, dt_bias, a_log,
              d_skip, ssd_norm_g, w_ssd_out, q_norm_g, k_norm_g, w_attn_out, w_out):
    for i in range(DEPTH):
        x, ctx = hybrid_layer(x, ctx, c, c_ctx, w_ada[i], b_ada[i], norm_g[i], w_in[i], conv_w[i],
                              conv_b[i], dt_bias[i], a_log[i], d_skip[i], ssd_norm_g[i], w_ssd_out[i],
                              q_norm_g[i], k_norm_g[i], w_attn_out[i], w_out[i],
                              update_ctx=(i < DEPTH - 1))
    return x
```

```python
import functools
import math

import jax
import jax.numpy as jnp
import numpy as np
from jax import lax
from jax.experimental import pallas as pl
from jax.experimental.pallas import tpu as pltpu

F32 = jnp.float32
BF16 = jnp.bfloat16

D_MODEL = 1024
D_INNER = 2048
SSD_HEAD_DIM = 64
SSD_HEADS = 32
SSD_GROUPS = 4
SSD_STATE = 128
SSD_CONV_W = 5
SSD_CHUNK = 128
XBC_DIM = D_INNER + 2 * SSD_GROUPS * SSD_STATE
ATTN_HEAD_DIM = 64
ATTN_HEADS = 16
ATTN_KV_HEADS = 4
ATTN_DIM = ATTN_HEADS * ATTN_HEAD_DIM
KV_DIM = ATTN_KV_HEADS * ATTN_HEAD_DIM
GRID_W = 64
ROPE_THETA = 10000.0
NORM_EPS = 1e-6

LANES = 128
BF16_SUBLANES = 16
VMEM_LIMIT_BYTES = 56 * 1024 * 1024

COL_Z = 0
COL_XBC = COL_Z + D_INNER
COL_Q = COL_XBC + XBC_DIM
COL_GA = COL_Q + ATTN_DIM
COL_GS = COL_GA + ATTN_DIM
COL_GATT = COL_GS + D_MODEL
COL_K = COL_GATT + D_MODEL
COL_V = COL_K + KV_DIM
PROJ_USED = COL_V + KV_DIM
PROJ_TN = 1408
PROJ_DIM = 7 * PROJ_TN
TT = 256
NEG_BIG = -1e30


def _cparams(sem, vmem=VMEM_LIMIT_BYTES):
    return pltpu.CompilerParams(dimension_semantics=sem, vmem_limit_bytes=vmem)


def _split2(a):
    hi = a.astype(BF16)
    lo = (a - hi.astype(F32)).astype(BF16)
    return hi, lo


def _split3(a):
    hi = a.astype(BF16)
    r = a - hi.astype(F32)
    mid = r.astype(BF16)
    lo = (r - mid.astype(F32)).astype(BF16)
    return hi, mid, lo


def _dot(a, b):
    return jnp.dot(a, b, preferred_element_type=F32)


def _dot_nt(a, b):
    return lax.dot_general(a, b, (((1,), (1,)), ((), ())), preferred_element_type=F32)


def _dot_tn(a, b):
    return lax.dot_general(a, b, (((0,), (0,)), ((), ())), preferred_element_type=F32)


def _mod_kernel(c_ref, w_ref, b_ref, o_ref):
    c = c_ref[...]
    s = c * jax.nn.sigmoid(c)
    s_hi, s_lo = _split2(s)
    w_hi, w_lo = _split2(w_ref[...])
    acc = _dot(s_hi, w_hi) + _dot(s_hi, w_lo) + _dot(s_lo, w_hi)
    o_ref[...] = acc + b_ref[...]


def _modulation(cc, w_ada, b_ada):
    rows = cc.shape[0]
    n = w_ada.shape[1]
    tn = 1024
    return pl.pallas_call(
        _mod_kernel,
        grid=(n // tn,),
        in_specs=[
            pl.BlockSpec((rows, D_MODEL), lambda j: (0, 0)),
            pl.BlockSpec((D_MODEL, tn), lambda j: (0, j)),
            pl.BlockSpec((1, tn), lambda j: (0, j)),
        ],
        out_specs=pl.BlockSpec((rows, tn), lambda j: (0, j)),
        out_shape=jax.ShapeDtypeStruct((rows, n), F32),
        compiler_params=_cparams(("arbitrary",)),
        name="adaln_mod",
    )(cc, w_ada, b_ada)


def _norm_kernel(x_ref, ctx_ref, mod_ref, g_ref, wdt_hi_ref, wdt_lo_ref, dtb_ref, a_ref,
                 hx_ref, dt_ref, *, nct):
    t = pl.program_id(1)
    xin = jnp.where(t < nct, ctx_ref[0], x_ref[0])
    ms = jnp.mean(xin * xin, axis=-1, keepdims=True)
    y = xin * lax.rsqrt(ms + NORM_EPS) * g_ref[...]
    m = mod_ref[0]
    shift = m[:, :D_MODEL]
    scale = m[:, D_MODEL:2 * D_MODEL]
    h = y * (1.0 + scale) + shift
    h_hi, h_lo = _split2(h)
    hx_ref[0] = h_hi
    raw = (_dot(h_hi, wdt_hi_ref[...]) + _dot(h_hi, wdt_lo_ref[...])
           + _dot(h_lo, wdt_hi_ref[...])) + dtb_ref[...]
    dt = jnp.maximum(raw, 0.0) + jnp.log1p(jnp.exp(-jnp.abs(raw)))
    dta = dt * a_ref[...]
    half = LANES // 2
    dt_ref[0] = jnp.concatenate([dt[:, :half], dta[:, :half]], axis=1)


def _norm_tokens(x, ctx, mod3, norm_g, wdt_hi, wdt_lo, dt_bias, a_neg):
    b, l, _ = x.shape
    lc = ctx.shape[1]
    t_all = lc + l
    nct = lc // TT
    nt = t_all // TT
    return pl.pallas_call(
        functools.partial(_norm_kernel, nct=nct),
        grid=(b, nt),
        in_specs=[
            pl.BlockSpec((1, TT, D_MODEL), lambda i, t: (i, jnp.maximum(t - nct, 0), 0)),
            pl.BlockSpec((1, TT, D_MODEL), lambda i, t: (i, jnp.minimum(t, nct - 1), 0)),
            pl.BlockSpec((1, 1, 3 * D_MODEL), lambda i, t: (jnp.where(t < nct, b, i), 0, 0)),
            pl.BlockSpec((1, D_MODEL), lambda i, t: (0, 0)),
            pl.BlockSpec((D_MODEL, LANES), lambda i, t: (0, 0)),
            pl.BlockSpec((D_MODEL, LANES), lambda i, t: (0, 0)),
            pl.BlockSpec((1, LANES), lambda i, t: (0, 0)),
            pl.BlockSpec((1, LANES), lambda i, t: (0, 0)),
        ],
        out_specs=[
            pl.BlockSpec((1, TT, D_MODEL), lambda i, t: (i, t, 0)),
            pl.BlockSpec((1, TT, LANES), lambda i, t: (i, t, 0)),
        ],
        out_shape=[
            jax.ShapeDtypeStruct((b, t_all, D_MODEL), BF16),
            jax.ShapeDtypeStruct((b, t_all, LANES), F32),
        ],
        compiler_params=_cparams(("parallel", "arbitrary")),
        name="norm_mod_dt",
    )(x, ctx, mod3, norm_g, wdt_hi, wdt_lo, dt_bias, a_neg)


def _proj_kernel(h_ref, w_ref, o_ref):
    o_ref[...] = _dot(h_ref[...], w_ref[...]).astype(o_ref.dtype)


def _pick_tile(n, candidates):
    for c in candidates:
        if n % c == 0:
            return c
    raise ValueError(f"no tile for {n}")


def _in_projection(hx2, w_r):
    m = hx2.shape[0]
    tm = _pick_tile(m, (1024, 512, 256))
    n = w_r.shape[1]
    return pl.pallas_call(
        _proj_kernel,
        grid=(n // PROJ_TN, m // tm),
        in_specs=[
            pl.BlockSpec((tm, D_MODEL), lambda j, i: (i, 0)),
            pl.BlockSpec((D_MODEL, PROJ_TN), lambda j, i: (0, j)),
        ],
        out_specs=pl.BlockSpec((tm, PROJ_TN), lambda j, i: (i, j)),
        out_shape=jax.ShapeDtypeStruct((m, n), BF16),
        compiler_params=_cparams(("parallel", "arbitrary")),
        name="in_proj",
    )(hx2, w_r)


def _conv_kernel(cur_ref, prev_ref, next_ref, w_ref, b_ref, o_ref, *, nct, nt):
    t = pl.program_id(1)
    halo = BF16_SUBLANES
    cur = cur_ref[0].astype(F32)
    pv = prev_ref[0].astype(F32)
    nx = next_ref[0].astype(F32)
    first = jnp.logical_or(t == 0, t == nct)
    last = jnp.logical_or(t == nct - 1, t == nt - 1)
    pv = jnp.where(first, 0.0, pv)
    nx = jnp.where(last, 0.0, nx)
    ext = jnp.concatenate([pv, cur, nx], axis=0)
    rows = ext.shape[0]
    pad = SSD_CONV_W // 2
    acc = jnp.zeros_like(cur) + b_ref[...]
    for k in range(SSD_CONV_W):
        shift = (pad - k) % rows
        shifted = ext if shift == 0 else pltpu.roll(ext, shift, 0)
        acc = acc + w_ref[k:k + 1, :] * shifted[halo:halo + TT]
    o_ref[0] = (acc * jax.nn.sigmoid(acc)).astype(o_ref.dtype)


def _conv_silu(proj3, conv_w, conv_b, nct):
    b, t_all, _ = proj3.shape
    nt = t_all // TT
    cw = 1024
    ncb = XBC_DIM // cw
    cb0 = COL_XBC // cw
    hb = TT // BF16_SUBLANES
    nhb = t_all // BF16_SUBLANES
    return pl.pallas_call(
        functools.partial(_conv_kernel, nct=nct, nt=nt),
        grid=(b, nt, ncb),
        in_specs=[
            pl.BlockSpec((1, TT, cw), lambda i, t, c: (i, t, cb0 + c)),
            pl.BlockSpec((1, BF16_SUBLANES, cw),
                         lambda i, t, c: (i, jnp.maximum(t * hb - 1, 0), cb0 + c)),
            pl.BlockSpec((1, BF16_SUBLANES, cw),
                         lambda i, t, c: (i, jnp.minimum((t + 1) * hb, nhb - 1), cb0 + c)),
            pl.BlockSpec((SSD_CONV_W, cw), lambda i, t, c: (0, c)),
            pl.BlockSpec((1, cw), lambda i, t, c: (0, c)),
        ],
        out_specs=pl.BlockSpec((1, TT, cw), lambda i, t, c: (i, t, c)),
        out_shape=jax.ShapeDtypeStruct((b, t_all, XBC_DIM), BF16),
        compiler_params=_cparams(("parallel", "arbitrary", "arbitrary")),
        name="conv_silu",
    )(proj3, proj3, proj3, conv_w, conv_b)


def _qk_kernel(q_ref, k_ref, v_ref, cos_ref, sa_ref, sb_ref, gq_ref, gk_ref, gm_ref,
               qh_ref, kh_ref, vt_ref):
    cos = cos_ref[...]
    sa = sa_ref[...]
    sb = sb_ref[...]
    gm = gm_ref[...]
    hd = ATTN_HEAD_DIM
    quarter = hd // 4

    def normrope(u, g):
        hi, lo = _split2(u * u)
        ms = _dot(hi, gm) + _dot(lo, gm)
        y = u * lax.rsqrt(ms + NORM_EPS) * g
        return (y * cos + pltpu.roll(y, quarter, 1) * sa
                + pltpu.roll(y, LANES - quarter, 1) * sb)

    for s in range(ATTN_DIM // LANES):
        r = normrope(q_ref[0, :, s * LANES:(s + 1) * LANES].astype(F32), gq_ref[...])
        qh_ref[0, 2 * s] = r[:, :hd].astype(qh_ref.dtype)
        qh_ref[0, 2 * s + 1] = r[:, hd:].astype(qh_ref.dtype)
    for s in range(KV_DIM // LANES):
        r = normrope(k_ref[0, :, s * LANES:(s + 1) * LANES].astype(F32), gk_ref[...])
        kh_ref[0, 2 * s] = r[:, :hd].astype(kh_ref.dtype)
        kh_ref[0, 2 * s + 1] = r[:, hd:].astype(kh_ref.dtype)
    vt = v_ref[0].astype(F32).T
    for g in range(ATTN_KV_HEADS):
        vt_ref[0, g] = vt[g * hd:(g + 1) * hd, :].astype(vt_ref.dtype)


def _qk_prepare(proj3, cos_t, sa_t, sb_t, gq, gk, gmat):
    b, t_all, _ = proj3.shape
    nt = t_all // TT
    hd = ATTN_HEAD_DIM
    return pl.pallas_call(
        _qk_kernel,
        grid=(b, nt),
        in_specs=[
            pl.BlockSpec((1, TT, ATTN_DIM), lambda i, t: (i, t, COL_Q // ATTN_DIM)),
            pl.BlockSpec((1, TT, KV_DIM), lambda i, t: (i, t, COL_K // KV_DIM)),
            pl.BlockSpec((1, TT, KV_DIM), lambda i, t: (i, t, COL_V // KV_DIM)),
            pl.BlockSpec((TT, LANES), lambda i, t: (t, 0)),
            pl.BlockSpec((TT, LANES), lambda i, t: (t, 0)),
            pl.BlockSpec((TT, LANES), lambda i, t: (t, 0)),
            pl.BlockSpec((1, LANES), lambda i, t: (0, 0)),
            pl.BlockSpec((1, LANES), lambda i, t: (0, 0)),
            pl.BlockSpec((LANES, LANES), lambda i, t: (0, 0)),
        ],
        out_specs=[
            pl.BlockSpec((1, ATTN_HEADS, TT, hd), lambda i, t: (i, 0, t, 0)),
            pl.BlockSpec((1, ATTN_KV_HEADS, TT, hd), lambda i, t: (i, 0, t, 0)),
            pl.BlockSpec((1, ATTN_KV_HEADS, hd, TT), lambda i, t: (i, 0, 0, t)),
        ],
        out_shape=[
            jax.ShapeDtypeStruct((b, ATTN_HEADS, t_all, hd), BF16),
            jax.ShapeDtypeStruct((b, ATTN_KV_HEADS, t_all, hd), BF16),
            jax.ShapeDtypeStruct((b, ATTN_KV_HEADS, hd, t_all), BF16),
        ],
        compiler_params=_cparams(("parallel", "arbitrary")),
        name="qk_norm_rope",
    )(proj3, proj3, proj3, cos_t, sa_t, sb_t, gq, gk, gmat)


def _attn_kernel(q_ref, k_ref, vt_ref, o_ref, *, tq, tk, nk):
    rep = ATTN_HEADS // ATTN_KV_HEADS
    hd = ATTN_HEAD_DIM
    nq = rep * tq
    q = q_ref[0].reshape(nq, hd)

    def body(c, carry):
        m, l, acc = carry
        off = pl.multiple_of(c * tk, tk)
        kc = k_ref[0, 0, pl.ds(off, tk), :]
        vc = vt_ref[0, 0, :, pl.ds(off, tk)]
        s = _dot_nt(kc, q)
        m_new = jnp.maximum(m, jnp.max(s, axis=0, keepdims=True))
        alpha = jnp.exp2(m - m_new)
        p = jnp.exp2(s - m_new)
        l_new = alpha * l + jnp.sum(p, axis=0, keepdims=True)
        acc_new = alpha * acc + _dot(vc, p.astype(BF16))
        return m_new, l_new, acc_new

    m0 = jnp.full((1, nq), NEG_BIG, F32)
    l0 = jnp.zeros((1, nq), F32)
    acc0 = jnp.zeros((hd, nq), F32)
    _, l, acc = lax.fori_loop(0, nk, body, (m0, l0, acc0))
    o = acc / l
    o4 = jnp.concatenate([o[:, j * tq:(j + 1) * tq] for j in range(rep)], axis=0)
    o_ref[0] = o4.T.astype(o_ref.dtype)


def _attention(qh, kh, vt, l, lc):
    b = qh.shape[0]
    t_all = kh.shape[2]
    hd = ATTN_HEAD_DIM
    rep = ATTN_HEADS // ATTN_KV_HEADS
    tq = 256
    tk = _pick_tile(t_all, (768, 512, 256))
    nk = t_all // tk
    q0 = lc // tq
    return pl.pallas_call(
        functools.partial(_attn_kernel, tq=tq, tk=tk, nk=nk),
        grid=(b, ATTN_KV_HEADS, l // tq),
        in_specs=[
            pl.BlockSpec((1, rep, tq, hd), lambda i, g, j: (i, g, q0 + j, 0)),
            pl.BlockSpec((1, 1, t_all, hd), lambda i, g, j: (i, g, 0, 0)),
            pl.BlockSpec((1, 1, hd, t_all), lambda i, g, j: (i, g, 0, 0)),
        ],
        out_specs=pl.BlockSpec((1, tq, rep * hd), lambda i, g, j: (i, j, g)),
        out_shape=jax.ShapeDtypeStruct((b, l, ATTN_DIM), BF16),
        compiler_params=_cparams(("parallel", "arbitrary", "arbitrary")),
        name="gqa_attention",
    )(qh, kh, vt)


def _ssd_kernel(x_ref, b_ref, c_ref, dt_ref, e_ref, y_ref, s_ref, *, reverse):
    q = SSD_CHUNK
    hpg = SSD_HEADS // SSD_GROUPS
    p = SSD_HEAD_DIM
    gw = hpg * p
    d = 1 if reverse else 0

    @pl.when(pl.program_id(1) == 0)
    def _():
        s_ref[...] = jnp.zeros_like(s_ref)

    ii = lax.broadcasted_iota(jnp.int32, (q, q), 0)
    jj = lax.broadcasted_iota(jnp.int32, (q, q), 1)
    mask = (jj >= ii) if reverse else (jj <= ii)
    tri = jnp.where(mask, 1.0, 0.0).astype(BF16)

    dtb = dt_ref[0]
    p1, p2, p3 = _split3(dtb)
    cum = _dot(tri, p1) + _dot(tri, p2) + _dot(tri, p3)
    cum_t = cum.T
    lo = 2 * SSD_HEADS + d * SSD_HEADS
    a_cum = cum[:, lo:lo + SSD_HEADS]
    a_cum_t = cum_t[lo:lo + SSD_HEADS, :]
    dt = dtb[:, d * SSD_HEADS:(d + 1) * SSD_HEADS]
    total = a_cum[0:1, :] if reverse else a_cum[q - 1:q, :]
    w_state = jnp.exp(total - a_cum) * dt
    ea = jnp.exp(a_cum)
    npad = LANES - 3 * SSD_HEADS
    lhs = jnp.concatenate([dt, w_state, ea, jnp.zeros((q, npad), F32)], axis=1).astype(BF16)
    ex = _dot(lhs, e_ref[...])
    dt_x = ex[:, :D_INNER]
    w_x = ex[:, D_INNER:2 * D_INNER]
    ea_x = ex[:, 2 * D_INNER:]
    tot = jnp.concatenate([jnp.broadcast_to(jnp.exp(total), (BF16_SUBLANES, SSD_HEADS)),
                           jnp.zeros((BF16_SUBLANES, LANES - SSD_HEADS), F32)], axis=1)
    t_hi, t_lo = _split2(tot)
    e1 = e_ref[:, :D_INNER]
    tot_x = (_dot(t_hi, e1) + _dot(t_lo, e1))[0:1, :]

    x = x_ref[0].astype(F32)
    xdt = (x * dt_x).astype(BF16)
    xw = (x * w_x).astype(BF16)

    for g in range(SSD_GROUPS):
        bg = b_ref[0, :, g * SSD_STATE:(g + 1) * SSD_STATE]
        cg = c_ref[0, :, g * SSD_STATE:(g + 1) * SSD_STATE]
        cb = _dot_nt(cg, bg)
        sg = s_ref[g]
        y_off = _dot(cg, sg.astype(BF16)) * ea_x[:, g * gw:(g + 1) * gw]
        ys = []
        for r in range(hpg):
            h = g * hpg + r
            seg = a_cum[:, h:h + 1] - a_cum_t[h:h + 1, :]
            dec = jnp.exp(jnp.where(mask, seg, NEG_BIG))
            w = (cb * dec).astype(BF16)
            ys.append(_dot(w, xdt[:, h * p:(h + 1) * p]))
        y_diag = jnp.concatenate(ys, axis=1)
        y_ref[0, :, g * gw:(g + 1) * gw] = (y_diag + y_off).astype(y_ref.dtype)
        s_ref[g] = tot_x[:, g * gw:(g + 1) * gw] * sg + _dot_tn(bg, xw[:, g * gw:(g + 1) * gw])


def _ssd_scan(xbc_act, dtv, e_mat, lc, reverse):
    b, t_all, _ = xbc_act.shape
    q = SSD_CHUNK
    nc = t_all // q
    ncc = lc // q
    sw = SSD_GROUPS * SSD_STATE

    def cidx(c):
        if not reverse:
            return c
        return jnp.where(c < ncc, ncc - 1 - c, nc - 1 + ncc - c)

    return pl.pallas_call(
        functools.partial(_ssd_kernel, reverse=reverse),
        grid=(b, nc),
        in_specs=[
            pl.BlockSpec((1, q, D_INNER), lambda i, c: (i, cidx(c), 0)),
            pl.BlockSpec((1, q, sw), lambda i, c: (i, cidx(c), D_INNER // sw)),
            pl.BlockSpec((1, q, sw), lambda i, c: (i, cidx(c), D_INNER // sw + 1)),
            pl.BlockSpec((1, q, LANES), lambda i, c: (i, cidx(c), 0)),
            pl.BlockSpec((LANES, 3 * D_INNER), lambda i, c: (0, 0)),
        ],
        out_specs=pl.BlockSpec((1, q, D_INNER), lambda i, c: (i, cidx(c), 0)),
        out_shape=jax.ShapeDtypeStruct((b, t_all, D_INNER), BF16),
        scratch_shapes=[pltpu.VMEM((SSD_GROUPS, SSD_STATE, D_INNER // SSD_GROUPS), F32)],
        compiler_params=_cparams(("parallel", "arbitrary")),
        name="ssd_scan_bwd" if reverse else "ssd_scan_fwd",
    )(xbc_act, xbc_act, xbc_act, dtv, e_mat)


def _merge_kernel(yf_ref, yb_ref, xs_ref, z_ref, o_ref, ga_ref, gs_ref, gatt_ref, x_ref,
                  mod_ref, dsk_ref, sng_ref, wssd_ref, watt_ref, wout_ref, out_ref):
    xs = xs_ref[0].astype(F32)
    y = yf_ref[0].astype(F32) + yb_ref[0].astype(F32) + dsk_ref[...] * xs
    z = z_ref[0].astype(F32)
    u = y * (z * jax.nn.sigmoid(z))
    gw = D_INNER // SSD_GROUPS
    parts = []
    for g in range(SSD_GROUPS):
        ug = u[:, g * gw:(g + 1) * gw]
        ms = jnp.mean(ug * ug, axis=-1, keepdims=True)
        parts.append(ug * lax.rsqrt(ms + NORM_EPS))
    un = (jnp.concatenate(parts, axis=1) * sng_ref[...]).astype(BF16)
    br_ssd = _dot(un, wssd_ref[...])
    ga = ga_ref[0].astype(F32)
    oa = (o_ref[0].astype(F32) * (ga * jax.nn.sigmoid(ga))).astype(BF16)
    br_att = _dot(oa, watt_ref[...])
    mix = (jax.nn.sigmoid(gs_ref[0].astype(F32)) * br_ssd
           + jax.nn.sigmoid(gatt_ref[0].astype(F32)) * br_att)
    out = _dot(mix.astype(BF16), wout_ref[...])
    gate = mod_ref[0][:, 2 * D_MODEL:]
    out_ref[0] = x_ref[0] + gate * out


def _merge(y_f, y_b, xbc_act, proj3, o, x, mod3, dskip_x, sng, w_ssd, w_att, w_out, lc):
    b, l, _ = x.shape
    tm = TT
    t0 = lc // tm
    full = lambda shape: pl.BlockSpec(shape, lambda i, j: tuple(0 for _ in shape))
    tok = lambda w, col: pl.BlockSpec((1, tm, w), lambda i, j: (i, t0 + j, col))
    return pl.pallas_call(
        _merge_kernel,
        grid=(b, l // tm),
        in_specs=[
            tok(D_INNER, 0),
            tok(D_INNER, 0),
            tok(D_INNER, 0),
            tok(D_INNER, COL_Z // D_INNER),
            pl.BlockSpec((1, tm, ATTN_DIM), lambda i, j: (i, j, 0)),
            tok(ATTN_DIM, COL_GA // ATTN_DIM),
            tok(D_MODEL, COL_GS // D_MODEL),
            tok(D_MODEL, COL_GATT // D_MODEL),
            pl.BlockSpec((1, tm, D_MODEL), lambda i, j: (i, j, 0)),
            pl.BlockSpec((1, 1, 3 * D_MODEL), lambda i, j: (i, 0, 0)),
            full((1, D_INNER)),
            full((1, D_INNER)),
            full((D_INNER, D_MODEL)),
            full((ATTN_DIM, D_MODEL)),
            full((D_MODEL, D_MODEL)),
        ],
        out_specs=pl.BlockSpec((1, tm, D_MODEL), lambda i, j: (i, j, 0)),
        out_shape=jax.ShapeDtypeStruct((b, l, D_MODEL), F32),
        compiler_params=_cparams(("parallel", "arbitrary")),
        name="merge_out",
    )(y_f, y_b, xbc_act, proj3, o, proj3, proj3, proj3, x, mod3, dskip_x, sng,
      w_ssd, w_att, w_out)


def _rope_tables(l, lc):
    rows = l // GRID_W
    half = ATTN_HEAD_DIM // 2
    nf = half // 2
    row = jnp.repeat(jnp.arange(rows, dtype=F32), GRID_W)
    col = jnp.tile(jnp.arange(GRID_W, dtype=F32), rows)
    inv_freq = ROPE_THETA ** (-(jnp.arange(nf, dtype=F32) / nf))
    ang_r = row[:, None] * inv_freq
    ang_c = col[:, None] * inv_freq
    zeros = jnp.zeros((l, nf), F32)
    cos_h = jnp.concatenate([jnp.cos(ang_r)] * 2 + [jnp.cos(ang_c)] * 2, axis=1)
    sa_h = jnp.concatenate([zeros, jnp.sin(ang_r), zeros, jnp.sin(ang_c)], axis=1)
    sb_h = jnp.concatenate([-jnp.sin(ang_r), zeros, -jnp.sin(ang_c), zeros], axis=1)
    ident = jnp.ones((lc, ATTN_HEAD_DIM), F32)
    nil = jnp.zeros((lc, ATTN_HEAD_DIM), F32)
    two = lambda a, pre: jnp.tile(jnp.concatenate([pre, a], axis=0), (1, LANES // ATTN_HEAD_DIM))
    return two(cos_h, ident), two(sa_h, nil), two(sb_h, nil)


def kernel(x, c, ctx, c_ctx, w_ada, b_ada, norm_g, w_in, conv_w, conv_b, dt_bias, a_log,
           d_skip, ssd_norm_g, w_ssd_out, q_norm_g, k_norm_g, w_attn_out, w_out):
    depth = w_ada.shape[0]
    assert depth == 1, "context outputs are only produced for a single layer"
    b, l, _ = x.shape
    lc = ctx.shape[1]
    assert l % TT == 0 and lc % TT == 0 and l % GRID_W == 0
    t_all = lc + l

    mod_rows = -(-(b + 1) // BF16_SUBLANES) * BF16_SUBLANES
    cc = jnp.zeros((mod_rows, D_MODEL), F32).at[:b].set(c).at[b].set(c_ctx)
    mod = _modulation(cc, w_ada[0], b_ada[0][None, :])
    mod3 = mod[:, None, :]

    w0 = w_in[0]
    o_z, o_xbc, o_dt, o_q, o_k, o_v, o_ga, o_gs, o_gatt = np.cumsum(
        [0, D_INNER, XBC_DIM, 2 * SSD_HEADS, ATTN_DIM, KV_DIM, KV_DIM, ATTN_DIM, D_MODEL])
    seg = lambda s, w: w0[:, s:s + w]
    w_r = jnp.concatenate([
        seg(o_z, D_INNER), seg(o_xbc, XBC_DIM), seg(o_q, ATTN_DIM), seg(o_ga, ATTN_DIM),
        seg(o_gs, D_MODEL), seg(o_gatt, D_MODEL), seg(o_k, KV_DIM), seg(o_v, KV_DIM),
        jnp.zeros((D_MODEL, PROJ_DIM - PROJ_USED), F32)], axis=1).astype(BF16)
    w_dt = jnp.pad(seg(o_dt, 2 * SSD_HEADS), ((0, 0), (0, LANES - 2 * SSD_HEADS)))
    wdt_hi = w_dt.astype(BF16)
    wdt_lo = (w_dt - wdt_hi.astype(F32)).astype(BF16)
    pad_lane = lambda v: jnp.pad(v.reshape(1, -1), ((0, 0), (0, LANES - v.size)))
    dtb = pad_lane(dt_bias[0].astype(F32))
    a_neg = pad_lane(-jnp.exp(a_log[0].astype(F32)))

    hx, dtv = _norm_tokens(x, ctx, mod3, norm_g[0][None, :], wdt_hi, wdt_lo, dtb, a_neg)
    proj = _in_projection(hx.reshape(b * t_all, D_MODEL), w_r)
    proj3 = proj.reshape(b, t_all, PROJ_DIM)

    xbc_act = _conv_silu(proj3, conv_w[0], conv_b[0][None, :], lc // TT)
    e_head = jnp.repeat(jnp.eye(SSD_HEADS, dtype=F32), SSD_HEAD_DIM, axis=1)
    e_mat = jnp.pad(jnp.kron(jnp.eye(3, dtype=F32), e_head),
                    ((0, LANES - 3 * SSD_HEADS), (0, 0))).astype(BF16)
    y_f = _ssd_scan(xbc_act, dtv, e_mat, lc, reverse=False)
    y_b = _ssd_scan(xbc_act, dtv, e_mat, lc, reverse=True)

    cos_t, sa_t, sb_t = _rope_tables(l, lc)
    rep2 = LANES // ATTN_HEAD_DIM
    q_scale = ATTN_HEAD_DIM ** -0.5 * math.log2(math.e)
    gq = jnp.tile(q_norm_g[0].astype(F32) * q_scale, rep2)[None, :]
    gk = jnp.tile(k_norm_g[0].astype(F32), rep2)[None, :]
    gmat = jnp.kron(jnp.eye(rep2, dtype=F32),
                    jnp.full((ATTN_HEAD_DIM, ATTN_HEAD_DIM), 1.0 / ATTN_HEAD_DIM, F32)).astype(BF16)
    qh, kh, vt = _qk_prepare(proj3, cos_t, sa_t, sb_t, gq, gk, gmat)
    o = _attention(qh, kh, vt, l, lc)

    dskip_x = jnp.repeat(d_skip[0].astype(F32), SSD_HEAD_DIM)[None, :]
    return _merge(y_f, y_b, xbc_act, proj3, o, x, mod3, dskip_x,
                  ssd_norm_g[0].astype(F32)[None, :], w_ssd_out[0].astype(BF16),
                  w_attn_out[0].astype(BF16), w_out[0].astype(BF16), lc)
```

```python
import functools
import math

import jax
import jax.numpy as jnp
import numpy as np
from jax import lax
from jax.experimental import pallas as pl
from jax.experimental.pallas import tpu as pltpu

F32 = jnp.float32
BF16 = jnp.bfloat16

D_MODEL = 1024
D_INNER = 2048
SSD_HEAD_DIM = 64
SSD_HEADS = 32
SSD_GROUPS = 4
SSD_STATE = 128
SSD_CONV_W = 5
SSD_CHUNK = 128
XBC_DIM = D_INNER + 2 * SSD_GROUPS * SSD_STATE
ATTN_HEAD_DIM = 64
ATTN_HEADS = 16
ATTN_KV_HEADS = 4
ATTN_DIM = ATTN_HEADS * ATTN_HEAD_DIM
KV_DIM = ATTN_KV_HEADS * ATTN_HEAD_DIM
GRID_W = 64
ROPE_THETA = 10000.0
NORM_EPS = 1e-6

LANES = 128
BF16_SUBLANES = 16
VMEM_LIMIT_BYTES = 56 * 1024 * 1024

COL_Z = 0
COL_XBC = COL_Z + D_INNER
COL_Q = COL_XBC + XBC_DIM
COL_GA = COL_Q + ATTN_DIM
COL_GS = COL_GA + ATTN_DIM
COL_GATT = COL_GS + D_MODEL
COL_K = COL_GATT + D_MODEL
COL_V = COL_K + KV_DIM
PROJ_USED = COL_V + KV_DIM
PROJ_TN = 1408
PROJ_DIM = 7 * PROJ_TN
TT = 256
VT_ROWS = ATTN_HEAD_DIM + BF16_SUBLANES
NEG_BIG = -1e30


def _cparams(sem, vmem=VMEM_LIMIT_BYTES):
    return pltpu.CompilerParams(dimension_semantics=sem, vmem_limit_bytes=vmem)


def _split2(a):
    hi = a.astype(BF16)
    lo = (a - hi.astype(F32)).astype(BF16)
    return hi, lo


def _split3(a):
    hi = a.astype(BF16)
    r = a - hi.astype(F32)
    mid = r.astype(BF16)
    lo = (r - mid.astype(F32)).astype(BF16)
    return hi, mid, lo


def _dot(a, b):
    return jnp.dot(a, b, preferred_element_type=F32)


def _dot_nt(a, b):
    return lax.dot_general(a, b, (((1,), (1,)), ((), ())), preferred_element_type=F32)


def _dot_tn(a, b):
    return lax.dot_general(a, b, (((0,), (0,)), ((), ())), preferred_element_type=F32)


def _mod_kernel(c_ref, w_ref, b_ref, o_ref):
    c = c_ref[...]
    s = c * jax.nn.sigmoid(c)
    s_hi, s_lo = _split2(s)
    w_hi, w_lo = _split2(w_ref[...])
    acc = _dot(s_hi, w_hi) + _dot(s_hi, w_lo) + _dot(s_lo, w_hi)
    o_ref[...] = acc + b_ref[...]


def _modulation(cc, w_ada, b_ada):
    rows = cc.shape[0]
    n = w_ada.shape[1]
    tn = 1024
    return pl.pallas_call(
        _mod_kernel,
        grid=(n // tn,),
        in_specs=[
            pl.BlockSpec((rows, D_MODEL), lambda j: (0, 0)),
            pl.BlockSpec((D_MODEL, tn), lambda j: (0, j)),
            pl.BlockSpec((1, tn), lambda j: (0, j)),
        ],
        out_specs=pl.BlockSpec((rows, tn), lambda j: (0, j)),
        out_shape=jax.ShapeDtypeStruct((rows, n), F32),
        compiler_params=_cparams(("arbitrary",)),
        name="adaln_mod",
    )(cc, w_ada, b_ada)


def _norm_kernel(x_ref, ctx_ref, mod_ref, g_ref, wdt_hi_ref, wdt_lo_ref, dtb_ref, a_ref,
                 hx_ref, dt_ref, *, nct):
    t = pl.program_id(1)
    xin = jnp.where(t < nct, ctx_ref[0], x_ref[0])
    ms = jnp.mean(xin * xin, axis=-1, keepdims=True)
    y = xin * lax.rsqrt(ms + NORM_EPS) * g_ref[...]
    m = mod_ref[0]
    shift = m[:, :D_MODEL]
    scale = m[:, D_MODEL:2 * D_MODEL]
    h = y * (1.0 + scale) + shift
    h_hi, h_lo = _split2(h)
    hx_ref[0] = h_hi
    raw = (_dot(h_hi, wdt_hi_ref[...]) + _dot(h_hi, wdt_lo_ref[...])
           + _dot(h_lo, wdt_hi_ref[...])) + dtb_ref[...]
    dt = jnp.maximum(raw, 0.0) + jnp.log1p(jnp.exp(-jnp.abs(raw)))
    dta = dt * a_ref[...]
    half = LANES // 2
    dt_ref[0] = jnp.concatenate([dt[:, :half], dta[:, :half]], axis=1)


def _norm_tokens(x, ctx, mod3, norm_g, wdt_hi, wdt_lo, dt_bias, a_neg):
    b, l, _ = x.shape
    lc = ctx.shape[1]
    t_all = lc + l
    nct = lc // TT
    nt = t_all // TT
    return pl.pallas_call(
        functools.partial(_norm_kernel, nct=nct),
        grid=(b, nt),
        in_specs=[
            pl.BlockSpec((1, TT, D_MODEL), lambda i, t: (i, jnp.maximum(t - nct, 0), 0)),
            pl.BlockSpec((1, TT, D_MODEL), lambda i, t: (i, jnp.minimum(t, nct - 1), 0)),
            pl.BlockSpec((1, 1, 3 * D_MODEL), lambda i, t: (jnp.where(t < nct, b, i), 0, 0)),
            pl.BlockSpec((1, D_MODEL), lambda i, t: (0, 0)),
            pl.BlockSpec((D_MODEL, LANES), lambda i, t: (0, 0)),
            pl.BlockSpec((D_MODEL, LANES), lambda i, t: (0, 0)),
            pl.BlockSpec((1, LANES), lambda i, t: (0, 0)),
            pl.BlockSpec((1, LANES), lambda i, t: (0, 0)),
        ],
        out_specs=[
            pl.BlockSpec((1, TT, D_MODEL), lambda i, t: (i, t, 0)),
            pl.BlockSpec((1, TT, LANES), lambda i, t: (i, t, 0)),
        ],
        out_shape=[
            jax.ShapeDtypeStruct((b, t_all, D_MODEL), BF16),
            jax.ShapeDtypeStruct((b, t_all, LANES), F32),
        ],
        compiler_params=_cparams(("parallel", "arbitrary")),
        name="norm_mod_dt",
    )(x, ctx, mod3, norm_g, wdt_hi, wdt_lo, dt_bias, a_neg)


def _proj_kernel(h_ref, w_ref, o_ref):
    o_ref[...] = _dot(h_ref[...], w_ref[...]).astype(o_ref.dtype)


def _pick_tile(n, candidates):
    for c in candidates:
        if n % c == 0:
            return c
    raise ValueError(f"no tile for {n}")


def _in_projection(hx2, w_r):
    m = hx2.shape[0]
    tm = _pick_tile(m, (1024, 512, 256))
    n = w_r.shape[1]
    return pl.pallas_call(
        _proj_kernel,
        grid=(n // PROJ_TN, m // tm),
        in_specs=[
            pl.BlockSpec((tm, D_MODEL), lambda j, i: (i, 0)),
            pl.BlockSpec((D_MODEL, PROJ_TN), lambda j, i: (0, j)),
        ],
        out_specs=pl.BlockSpec((tm, PROJ_TN), lambda j, i: (i, j)),
        out_shape=jax.ShapeDtypeStruct((m, n), BF16),
        compiler_params=_cparams(("parallel", "arbitrary")),
        name="in_proj",
    )(hx2, w_r)


def _conv_kernel(cur_ref, prev_ref, next_ref, w_ref, b_ref, o_ref, *, nct, nt):
    t = pl.program_id(1)
    halo = BF16_SUBLANES
    cur = cur_ref[0].astype(F32)
    pv = prev_ref[0].astype(F32)
    nx = next_ref[0].astype(F32)
    first = jnp.logical_or(t == 0, t == nct)
    last = jnp.logical_or(t == nct - 1, t == nt - 1)
    pv = jnp.where(first, 0.0, pv)
    nx = jnp.where(last, 0.0, nx)
    ext = jnp.concatenate([pv, cur, nx], axis=0)
    rows = ext.shape[0]
    pad = SSD_CONV_W // 2
    acc = jnp.zeros_like(cur) + b_ref[...]
    for k in range(SSD_CONV_W):
        shift = (pad - k) % rows
        shifted = ext if shift == 0 else pltpu.roll(ext, shift, 0)
        acc = acc + w_ref[k:k + 1, :] * shifted[halo:halo + TT]
    o_ref[0] = (acc * jax.nn.sigmoid(acc)).astype(o_ref.dtype)


def _conv_silu(proj3, conv_w, conv_b, nct):
    b, t_all, _ = proj3.shape
    nt = t_all // TT
    cw = 1024
    ncb = XBC_DIM // cw
    cb0 = COL_XBC // cw
    hb = TT // BF16_SUBLANES
    nhb = t_all // BF16_SUBLANES
    return pl.pallas_call(
        functools.partial(_conv_kernel, nct=nct, nt=nt),
        grid=(b, nt, ncb),
        in_specs=[
            pl.BlockSpec((1, TT, cw), lambda i, t, c: (i, t, cb0 + c)),
            pl.BlockSpec((1, BF16_SUBLANES, cw),
                         lambda i, t, c: (i, jnp.maximum(t * hb - 1, 0), cb0 + c)),
            pl.BlockSpec((1, BF16_SUBLANES, cw),
                         lambda i, t, c: (i, jnp.minimum((t + 1) * hb, nhb - 1), cb0 + c)),
            pl.BlockSpec((SSD_CONV_W, cw), lambda i, t, c: (0, c)),
            pl.BlockSpec((1, cw), lambda i, t, c: (0, c)),
        ],
        out_specs=pl.BlockSpec((1, TT, cw), lambda i, t, c: (i, t, c)),
        out_shape=jax.ShapeDtypeStruct((b, t_all, XBC_DIM), BF16),
        compiler_params=_cparams(("parallel", "arbitrary", "arbitrary")),
        name="conv_silu",
    )(proj3, proj3, proj3, conv_w, conv_b)


def _qk_kernel(q_ref, k_ref, v_ref, cos_ref, sa_ref, sb_ref, gq_ref, gk_ref, gm_ref,
               qh_ref, kh_ref, vt_ref):
    cos = cos_ref[...]
    sa = sa_ref[...]
    sb = sb_ref[...]
    gm = gm_ref[...]
    hd = ATTN_HEAD_DIM
    quarter = hd // 4

    def normrope(u, g):
        hi, lo = _split2(u * u)
        ms = _dot(hi, gm) + _dot(lo, gm)
        y = u * lax.rsqrt(ms + NORM_EPS) * g
        return (y * cos + pltpu.roll(y, quarter, 1) * sa
                + pltpu.roll(y, LANES - quarter, 1) * sb)

    for s in range(ATTN_DIM // LANES):
        r = normrope(q_ref[0, :, s * LANES:(s + 1) * LANES].astype(F32), gq_ref[...])
        qh_ref[0, 2 * s] = r[:, :hd].astype(qh_ref.dtype)
        qh_ref[0, 2 * s + 1] = r[:, hd:].astype(qh_ref.dtype)
    for s in range(KV_DIM // LANES):
        r = normrope(k_ref[0, :, s * LANES:(s + 1) * LANES].astype(F32), gk_ref[...])
        kh_ref[0, 2 * s] = r[:, :hd].astype(kh_ref.dtype)
        kh_ref[0, 2 * s + 1] = r[:, hd:].astype(kh_ref.dtype)
    vt = v_ref[0].astype(F32).T
    ones = jnp.ones((VT_ROWS - hd, vt.shape[1]), vt_ref.dtype)
    for g in range(ATTN_KV_HEADS):
        vt_ref[0, g, :hd, :] = vt[g * hd:(g + 1) * hd, :].astype(vt_ref.dtype)
        vt_ref[0, g, hd:, :] = ones


def _qk_prepare(proj3, cos_t, sa_t, sb_t, gq, gk, gmat):
    b, t_all, _ = proj3.shape
    nt = t_all // TT
    hd = ATTN_HEAD_DIM
    return pl.pallas_call(
        _qk_kernel,
        grid=(b, nt),
        in_specs=[
            pl.BlockSpec((1, TT, ATTN_DIM), lambda i, t: (i, t, COL_Q // ATTN_DIM)),
            pl.BlockSpec((1, TT, KV_DIM), lambda i, t: (i, t, COL_K // KV_DIM)),
            pl.BlockSpec((1, TT, KV_DIM), lambda i, t: (i, t, COL_V // KV_DIM)),
            pl.BlockSpec((TT, LANES), lambda i, t: (t, 0)),
            pl.BlockSpec((TT, LANES), lambda i, t: (t, 0)),
            pl.BlockSpec((TT, LANES), lambda i, t: (t, 0)),
            pl.BlockSpec((1, LANES), lambda i, t: (0, 0)),
            pl.BlockSpec((1, LANES), lambda i, t: (0, 0)),
            pl.BlockSpec((LANES, LANES), lambda i, t: (0, 0)),
        ],
        out_specs=[
            pl.BlockSpec((1, ATTN_HEADS, TT, hd), lambda i, t: (i, 0, t, 0)),
            pl.BlockSpec((1, ATTN_KV_HEADS, TT, hd), lambda i, t: (i, 0, t, 0)),
            pl.BlockSpec((1, ATTN_KV_HEADS, VT_ROWS, TT), lambda i, t: (i, 0, 0, t)),
        ],
        out_shape=[
            jax.ShapeDtypeStruct((b, ATTN_HEADS, t_all, hd), BF16),
            jax.ShapeDtypeStruct((b, ATTN_KV_HEADS, t_all, hd), BF16),
            jax.ShapeDtypeStruct((b, ATTN_KV_HEADS, VT_ROWS, t_all), BF16),
        ],
        compiler_params=_cparams(("parallel", "arbitrary")),
        name="qk_norm_rope",
    )(proj3, proj3, proj3, cos_t, sa_t, sb_t, gq, gk, gmat)


def _attn_kernel(q_ref, k_ref, vt_ref, o_ref, s_scr, p_scr, *, tq, tk, nk):
    rep = ATTN_HEADS // ATTN_KV_HEADS
    hd = ATTN_HEAD_DIM
    nq = rep * tq
    q = q_ref[0].reshape(nq, hd)

    def scores(c):
        s_scr[c % 2] = _dot_nt(k_ref[0, 0, c * tk:(c + 1) * tk, :], q)

    def colmax(c, m_prev):
        return jnp.maximum(m_prev, jnp.max(s_scr[c % 2], axis=0, keepdims=True))

    scores(0)
    m_prev = jnp.full((1, nq), NEG_BIG, F32)
    m = colmax(0, m_prev)
    acc = jnp.zeros((vt_ref.shape[2], nq), F32)
    for c in range(nk):
        if c + 1 < nk:
            scores(c + 1)
        p_scr[c % 2] = jnp.exp2(s_scr[c % 2] - m).astype(BF16)
        pv = _dot(vt_ref[0, 0, :, c * tk:(c + 1) * tk], p_scr[c % 2])
        acc = jnp.exp2(m_prev - m) * acc + pv
        m_prev = m
        if c + 1 < nk:
            m = colmax(c + 1, m)
    o = acc[:hd] / acc[hd:hd + 1]
    o4 = jnp.concatenate([o[:, j * tq:(j + 1) * tq] for j in range(rep)], axis=0)
    o_ref[0] = o4.T.astype(o_ref.dtype)


def _attention(qh, kh, vt, l, lc):
    b = qh.shape[0]
    t_all = kh.shape[2]
    hd = ATTN_HEAD_DIM
    rep = ATTN_HEADS // ATTN_KV_HEADS
    tq = 256
    tk = 256
    nk = t_all // tk
    q0 = lc // tq
    return pl.pallas_call(
        functools.partial(_attn_kernel, tq=tq, tk=tk, nk=nk),
        grid=(b, ATTN_KV_HEADS, l // tq),
        in_specs=[
            pl.BlockSpec((1, rep, tq, hd), lambda i, g, j: (i, g, q0 + j, 0)),
            pl.BlockSpec((1, 1, t_all, hd), lambda i, g, j: (i, g, 0, 0)),
            pl.BlockSpec((1, 1, VT_ROWS, t_all), lambda i, g, j: (i, g, 0, 0)),
        ],
        out_specs=pl.BlockSpec((1, tq, rep * hd), lambda i, g, j: (i, j, g)),
        out_shape=jax.ShapeDtypeStruct((b, l, ATTN_DIM), BF16),
        scratch_shapes=[pltpu.VMEM((2, tk, rep * tq), F32), pltpu.VMEM((2, tk, rep * tq), BF16)],
        compiler_params=_cparams(("parallel", "arbitrary", "arbitrary")),
        name="gqa_attention",
    )(qh, kh, vt)


def _ssd_kernel(x_ref, b_ref, c_ref, dt_ref, e_ref, y_ref, s_ref, *, reverse):
    q = SSD_CHUNK
    hpg = SSD_HEADS // SSD_GROUPS
    p = SSD_HEAD_DIM
    gw = hpg * p
    d = 1 if reverse else 0

    @pl.when(pl.program_id(1) == 0)
    def _():
        s_ref[...] = jnp.zeros_like(s_ref)

    ii = lax.broadcasted_iota(jnp.int32, (q, q), 0)
    jj = lax.broadcasted_iota(jnp.int32, (q, q), 1)
    mask = (jj >= ii) if reverse else (jj <= ii)
    tri = jnp.where(mask, 1.0, 0.0).astype(BF16)

    dtb = dt_ref[0]
    p1, p2, p3 = _split3(dtb)
    cum = _dot(tri, p1) + _dot(tri, p2) + _dot(tri, p3)
    cum_t = cum.T
    lo = 2 * SSD_HEADS + d * SSD_HEADS
    a_cum = cum[:, lo:lo + SSD_HEADS]
    a_cum_t = cum_t[lo:lo + SSD_HEADS, :]
    dt = dtb[:, d * SSD_HEADS:(d + 1) * SSD_HEADS]
    total = a_cum[0:1, :] if reverse else a_cum[q - 1:q, :]
    w_state = jnp.exp(total - a_cum) * dt
    ea = jnp.exp(a_cum)
    npad = LANES - 3 * SSD_HEADS
    lhs = jnp.concatenate([dt, w_state, ea, jnp.zeros((q, npad), F32)], axis=1).astype(BF16)
    ex = _dot(lhs, e_ref[...])
    dt_x = ex[:, :D_INNER]
    w_x = ex[:, D_INNER:2 * D_INNER]
    ea_x = ex[:, 2 * D_INNER:]
    tot = jnp.concatenate([jnp.broadcast_to(jnp.exp(total), (BF16_SUBLANES, SSD_HEADS)),
                           jnp.zeros((BF16_SUBLANES, LANES - SSD_HEADS), F32)], axis=1)
    t_hi, t_lo = _split2(tot)
    e1 = e_ref[:, :D_INNER]
    tot_x = (_dot(t_hi, e1) + _dot(t_lo, e1))[0:1, :]

    x = x_ref[0].astype(F32)
    xdt = (x * dt_x).astype(BF16)
    xw = (x * w_x).astype(BF16)

    for g in range(SSD_GROUPS):
        bg = b_ref[0, :, g * SSD_STATE:(g + 1) * SSD_STATE]
        cg = c_ref[0, :, g * SSD_STATE:(g + 1) * SSD_STATE]
        cb = _dot_nt(cg, bg)
        sg = s_ref[g]
        y_off = _dot(cg, sg.astype(BF16)) * ea_x[:, g * gw:(g + 1) * gw]
        ys = []
        for r in range(hpg):
            h = g * hpg + r
            seg = a_cum[:, h:h + 1] - a_cum_t[h:h + 1, :]
            dec = jnp.exp(jnp.where(mask, seg, NEG_BIG))
            w = (cb * dec).astype(BF16)
            ys.append(_dot(w, xdt[:, h * p:(h + 1) * p]))
        y_diag = jnp.concatenate(ys, axis=1)
        y_ref[0, :, g * gw:(g + 1) * gw] = (y_diag + y_off).astype(y_ref.dtype)
        s_ref[g] = tot_x[:, g * gw:(g + 1) * gw] * sg + _dot_tn(bg, xw[:, g * gw:(g + 1) * gw])


def _ssd_scan(xbc_act, dtv, e_mat, lc, reverse):
    b, t_all, _ = xbc_act.shape
    q = SSD_CHUNK
    nc = t_all // q
    ncc = lc // q
    sw = SSD_GROUPS * SSD_STATE

    def cidx(c):
        if not reverse:
            return c
        return jnp.where(c < ncc, ncc - 1 - c, nc - 1 + ncc - c)

    return pl.pallas_call(
        functools.partial(_ssd_kernel, reverse=reverse),
        grid=(b, nc),
        in_specs=[
            pl.BlockSpec((1, q, D_INNER), lambda i, c: (i, cidx(c), 0)),
            pl.BlockSpec((1, q, sw), lambda i, c: (i, cidx(c), D_INNER // sw)),
            pl.BlockSpec((1, q, sw), lambda i, c: (i, cidx(c), D_INNER // sw + 1)),
            pl.BlockSpec((1, q, LANES), lambda i, c: (i, cidx(c), 0)),
            pl.BlockSpec((LANES, 3 * D_INNER), lambda i, c: (0, 0)),
        ],
        out_specs=pl.BlockSpec((1, q, D_INNER), lambda i, c: (i, cidx(c), 0)),
        out_shape=jax.ShapeDtypeStruct((b, t_all, D_INNER), BF16),
        scratch_shapes=[pltpu.VMEM((SSD_GROUPS, SSD_STATE, D_INNER // SSD_GROUPS), F32)],
        compiler_params=_cparams(("parallel", "arbitrary")),
        name="ssd_scan_bwd" if reverse else "ssd_scan_fwd",
    )(xbc_act, xbc_act, xbc_act, dtv, e_mat)


def _merge_kernel(yf_ref, yb_ref, xs_ref, z_ref, o_ref, ga_ref, gs_ref, gatt_ref, x_ref,
                  mod_ref, dsk_ref, sng_ref, wssd_ref, watt_ref, wout_ref, out_ref):
    xs = xs_ref[0].astype(F32)
    y = yf_ref[0].astype(F32) + yb_ref[0].astype(F32) + dsk_ref[...] * xs
    z = z_ref[0].astype(F32)
    u = y * (z * jax.nn.sigmoid(z))
    gw = D_INNER // SSD_GROUPS
    parts = []
    for g in range(SSD_GROUPS):
        ug = u[:, g * gw:(g + 1) * gw]
        ms = jnp.mean(ug * ug, axis=-1, keepdims=True)
        parts.append(ug * lax.rsqrt(ms + NORM_EPS))
    un = (jnp.concatenate(parts, axis=1) * sng_ref[...]).astype(BF16)
    br_ssd = _dot(un, wssd_ref[...])
    ga = ga_ref[0].astype(F32)
    oa = (o_ref[0].astype(F32) * (ga * jax.nn.sigmoid(ga))).astype(BF16)
    br_att = _dot(oa, watt_ref[...])
    mix = (jax.nn.sigmoid(gs_ref[0].astype(F32)) * br_ssd
           + jax.nn.sigmoid(gatt_ref[0].astype(F32)) * br_att)
    out = _dot(mix.astype(BF16), wout_ref[...])
    gate = mod_ref[0][:, 2 * D_MODEL:]
    out_ref[0] = x_ref[0] + gate * out


def _merge(y_f, y_b, xbc_act, proj3, o, x, mod3, dskip_x, sng, w_ssd, w_att, w_out, lc):
    b, l, _ = x.shape
    tm = TT
    t0 = lc // tm
    full = lambda shape: pl.BlockSpec(shape, lambda i, j: tuple(0 for _ in shape))
    tok = lambda w, col: pl.BlockSpec((1, tm, w), lambda i, j: (i, t0 + j, col))
    return pl.pallas_call(
        _merge_kernel,
        grid=(b, l // tm),
        in_specs=[
            tok(D_INNER, 0),
            tok(D_INNER, 0),
            tok(D_INNER, 0),
            tok(D_INNER, COL_Z // D_INNER),
            pl.BlockSpec((1, tm, ATTN_DIM), lambda i, j: (i, j, 0)),
            tok(ATTN_DIM, COL_GA // ATTN_DIM),
            tok(D_MODEL, COL_GS // D_MODEL),
            tok(D_MODEL, COL_GATT // D_MODEL),
            pl.BlockSpec((1, tm, D_MODEL), lambda i, j: (i, j, 0)),
            pl.BlockSpec((1, 1, 3 * D_MODEL), lambda i, j: (i, 0, 0)),
            full((1, D_INNER)),
            full((1, D_INNER)),
            full((D_INNER, D_MODEL)),
            full((ATTN_DIM, D_MODEL)),
            full((D_MODEL, D_MODEL)),
        ],
        out_specs=pl.BlockSpec((1, tm, D_MODEL), lambda i, j: (i, j, 0)),
        out_shape=jax.ShapeDtypeStruct((b, l, D_MODEL), F32),
        compiler_params=_cparams(("parallel", "arbitrary")),
        name="merge_out",
    )(y_f, y_b, xbc_act, proj3, o, proj3, proj3, proj3, x, mod3, dskip_x, sng,
      w_ssd, w_att, w_out)


def _rope_tables(l, lc):
    rows = l // GRID_W
    half = ATTN_HEAD_DIM // 2
    nf = half // 2
    row = jnp.repeat(jnp.arange(rows, dtype=F32), GRID_W)
    col = jnp.tile(jnp.arange(GRID_W, dtype=F32), rows)
    inv_freq = ROPE_THETA ** (-(jnp.arange(nf, dtype=F32) / nf))
    ang_r = row[:, None] * inv_freq
    ang_c = col[:, None] * inv_freq
    zeros = jnp.zeros((l, nf), F32)
    cos_h = jnp.concatenate([jnp.cos(ang_r)] * 2 + [jnp.cos(ang_c)] * 2, axis=1)
    sa_h = jnp.concatenate([zeros, jnp.sin(ang_r), zeros, jnp.sin(ang_c)], axis=1)
    sb_h = jnp.concatenate([-jnp.sin(ang_r), zeros, -jnp.sin(ang_c), zeros], axis=1)
    ident = jnp.ones((lc, ATTN_HEAD_DIM), F32)
    nil = jnp.zeros((lc, ATTN_HEAD_DIM), F32)
    two = lambda a, pre: jnp.tile(jnp.concatenate([pre, a], axis=0), (1, LANES // ATTN_HEAD_DIM))
    return two(cos_h, ident), two(sa_h, nil), two(sb_h, nil)


def kernel(x, c, ctx, c_ctx, w_ada, b_ada, norm_g, w_in, conv_w, conv_b, dt_bias, a_log,
           d_skip, ssd_norm_g, w_ssd_out, q_norm_g, k_norm_g, w_attn_out, w_out):
    depth = w_ada.shape[0]
    assert depth == 1, "context outputs are only produced for a single layer"
    b, l, _ = x.shape
    lc = ctx.shape[1]
    assert l % TT == 0 and lc % TT == 0 and l % GRID_W == 0
    t_all = lc + l

    mod_rows = -(-(b + 1) // BF16_SUBLANES) * BF16_SUBLANES
    cc = jnp.zeros((mod_rows, D_MODEL), F32).at[:b].set(c).at[b].set(c_ctx)
    mod = _modulation(cc, w_ada[0], b_ada[0][None, :])
    mod3 = mod[:, None, :]

    w0 = w_in[0]
    o_z, o_xbc, o_dt, o_q, o_k, o_v, o_ga, o_gs, o_gatt = np.cumsum(
        [0, D_INNER, XBC_DIM, 2 * SSD_HEADS, ATTN_DIM, KV_DIM, KV_DIM, ATTN_DIM, D_MODEL])
    seg = lambda s, w: w0[:, s:s + w]
    w_r = jnp.concatenate([
        seg(o_z, D_INNER), seg(o_xbc, XBC_DIM), seg(o_q, ATTN_DIM), seg(o_ga, ATTN_DIM),
        seg(o_gs, D_MODEL), seg(o_gatt, D_MODEL), seg(o_k, KV_DIM), seg(o_v, KV_DIM),
        jnp.zeros((D_MODEL, PROJ_DIM - PROJ_USED), F32)], axis=1).astype(BF16)
    w_dt = jnp.pad(seg(o_dt, 2 * SSD_HEADS), ((0, 0), (0, LANES - 2 * SSD_HEADS)))
    wdt_hi = w_dt.astype(BF16)
    wdt_lo = (w_dt - wdt_hi.astype(F32)).astype(BF16)
    pad_lane = lambda v: jnp.pad(v.reshape(1, -1), ((0, 0), (0, LANES - v.size)))
    dtb = pad_lane(dt_bias[0].astype(F32))
    a_neg = pad_lane(-jnp.exp(a_log[0].astype(F32)))

    hx, dtv = _norm_tokens(x, ctx, mod3, norm_g[0][None, :], wdt_hi, wdt_lo, dtb, a_neg)
    proj = _in_projection(hx.reshape(b * t_all, D_MODEL), w_r)
    proj3 = proj.reshape(b, t_all, PROJ_DIM)

    xbc_act = _conv_silu(proj3, conv_w[0], conv_b[0][None, :], lc // TT)
    e_head = jnp.repeat(jnp.eye(SSD_HEADS, dtype=F32), SSD_HEAD_DIM, axis=1)
    e_mat = jnp.pad(jnp.kron(jnp.eye(3, dtype=F32), e_head),
                    ((0, LANES - 3 * SSD_HEADS), (0, 0))).astype(BF16)
    y_f = _ssd_scan(xbc_act, dtv, e_mat, lc, reverse=False)
    y_b = _ssd_scan(xbc_act, dtv, e_mat, lc, reverse=True)

    cos_t, sa_t, sb_t = _rope_tables(l, lc)
    rep2 = LANES // ATTN_HEAD_DIM
    q_scale = ATTN_HEAD_DIM ** -0.5 * math.log2(math.e)
    gq = jnp.tile(q_norm_g[0].astype(F32) * q_scale, rep2)[None, :]
    gk = jnp.tile(k_norm_g[0].astype(F32), rep2)[None, :]
    gmat = jnp.kron(jnp.eye(rep2, dtype=F32),
                    jnp.full((ATTN_HEAD_DIM, ATTN_HEAD_DIM), 1.0 / ATTN_HEAD_DIM, F32)).astype(BF16)
    qh, kh, vt = _qk_prepare(proj3, cos_t, sa_t, sb_t, gq, gk, gmat)
    o = _attention(qh, kh, vt, l, lc)

    dskip_x = jnp.repeat(d_skip[0].astype(F32), SSD_HEAD_DIM)[None, :]
    return _merge(y_f, y_b, xbc_act, proj3, o, x, mod3, dskip_x,
                  ssd_norm_g[0].astype(F32)[None, :], w_ssd_out[0].astype(BF16),
                  w_attn_out[0].astype(BF16), w_out[0].astype(BF16), lc)
```
